```python
import math
import jax
import jax.numpy as jnp
from jax import lax
import numpy as np

D_MODEL = 1024
BATCH = 4
SEQ = 4096
DEPTH = 4
DEC_BATCH = 128
DEC_SEQ = 8
PAST_LEN = 8192
PAGE_SIZE = 128

D_MIX = D_MODEL
HEAD_DIM = 64
W_FOX = D_MIX // 4
H_FOX = W_FOX // HEAD_DIM
W_MLA = D_MIX // 4
H_MLA = W_MLA // HEAD_DIM
NOPE_DIM = HEAD_DIM
ROPE_DIM = HEAD_DIM // 2
V_DIM = HEAD_DIM
KV_LORA_RANK = D_MODEL // 8
W_LRU = D_MIX - W_FOX - W_MLA
LRU_BLOCK = 64
N_LRU_BLOCKS = W_LRU // LRU_BLOCK
CONV_W = 4
LRU_C = 8.0
ROPE_BASE = 10000.0
Q_BLOCK = 128
FORGET_BIAS = 5.0
LN_EPS = 1e-5
RMS_EPS = 1e-6
NEG_INF = -1e30
DEEPNORM_ALPHA = (2 * DEPTH) ** 0.25
DEEPNORM_BETA = (8 * DEPTH) ** -0.25
D_IN = (4 * W_FOX + H_FOX) + (H_MLA * NOPE_DIM + H_MLA * ROPE_DIM + KV_LORA_RANK + ROPE_DIM + W_MLA) + 2 * W_LRU

kernel_name = "hymba_fox_mla_rglru_deepnorm_step"


def split_projection(z):
    sizes = (W_FOX, W_FOX, W_FOX, H_FOX, W_FOX,
             H_MLA * NOPE_DIM, H_MLA * ROPE_DIM, KV_LORA_RANK, ROPE_DIM, W_MLA,
             W_LRU, W_LRU)
    parts, start = [], 0
    for s in sizes:
        parts.append(z[..., start:start + s])
        start += s
    return parts


def layernorm(x, g, b):
    xf = x.astype(jnp.float32)
    mu = jnp.mean(xf, axis=-1, keepdims=True)
    xc = xf - mu
    var = jnp.mean(xc * xc, axis=-1, keepdims=True)
    return (xc * lax.rsqrt(var + LN_EPS) * g.astype(jnp.float32) + b.astype(jnp.float32)).astype(x.dtype)


def rmsnorm(x, g):
    xf = x.astype(jnp.float32)
    return (xf * lax.rsqrt(jnp.mean(xf * xf, axis=-1, keepdims=True) + RMS_EPS) * g.astype(jnp.float32)).astype(x.dtype)


def rope(x, pos):
    half = x.shape[-1] // 2
    inv = ROPE_BASE ** (-jnp.arange(half, dtype=jnp.float32) / half)
    ang = pos.astype(jnp.float32)[:, None] * inv[None, :]
    cos = jnp.cos(ang)[None, :, None, :]
    sin = jnp.sin(ang)[None, :, None, :]
    xf = x.astype(jnp.float32)
    x1, x2 = xf[..., :half], xf[..., half:]
    return jnp.concatenate([x1 * cos - x2 * sin, x1 * sin + x2 * cos], axis=-1).astype(x.dtype)


def fox_attention(q, k, v, c_q, c_k, q_pos, k_pos):
    s = jnp.einsum('bqhd,bkhd->bhqk', q, k).astype(jnp.float32) * (1.0 / math.sqrt(HEAD_DIM))
    s = s + (jnp.swapaxes(c_q, 1, 2)[..., :, None] - jnp.swapaxes(c_k, 1, 2)[..., None, :])
    mask = k_pos[None, :] <= q_pos[:, None]
    s = jnp.where(mask, s, NEG_INF)
    p = jax.nn.softmax(s, axis=-1).astype(v.dtype)
    return jnp.einsum('bhqk,bkhd->bqhd', p, v)


def mla_attention(q, kv, v_lat, q_pos, k_pos):
    s = jnp.einsum('bqhd,bkd->bhqk', q, kv).astype(jnp.float32) * (1.0 / math.sqrt(NOPE_DIM + ROPE_DIM))
    mask = k_pos[None, :] <= q_pos[:, None]
    s = jnp.where(mask, s, NEG_INF)
    p = jax.nn.softmax(s, axis=-1).astype(v_lat.dtype)
    return jnp.einsum('bhqk,bkr->bqhr', p, v_lat)


def sweep_queries(attn, q_parts, q_pos):
    T = q_pos.shape[0]
    if T <= Q_BLOCK or T % Q_BLOCK:
        return attn(*q_parts, q_pos)
    nb = T // Q_BLOCK
    blk = lambda a: jnp.moveaxis(a.reshape(a.shape[0], nb, Q_BLOCK, *a.shape[2:]), 1, 0)
    out = lax.map(lambda args: attn(*args), tuple(blk(a) for a in q_parts) + (q_pos.reshape(nb, Q_BLOCK),))
    out = jnp.moveaxis(out, 0, 1)
    return out.reshape(out.shape[0], T, *out.shape[3:])


def lru_combine(e1, e2):
    a1, b1 = e1
    a2, b2 = e2
    return a1 * a2, a2 * b1 + b2


def gather_pages(cache_l, page_table):
    pages = cache_l[page_table]
    return pages.reshape(pages.shape[0], pages.shape[1] * pages.shape[2], *pages.shape[3:])


def mixer_layer(x, pos, past, h0, conv_buf, w_in, b_f, g_kv, w_uk, w_uv, conv_w, conv_b,
                w_gate_a, b_gate_a, w_gate_x, b_gate_x, lru_lambda, w_out, ln_g, ln_b):
    B, T, _ = x.shape
    z = jnp.einsum('btd,de->bte', x, w_in)
    fq, fk, fv, ff, fg, mqn, mqr, mckv, mkr, mg, lx, lg = split_projection(z)

    fq = fq.reshape(B, T, H_FOX, HEAD_DIM)
    fk = fk.reshape(B, T, H_FOX, HEAD_DIM)
    fv = fv.reshape(B, T, H_FOX, HEAD_DIM)
    logf = jax.nn.log_sigmoid((ff + b_f).astype(jnp.float32))

    ckv = rmsnorm(mckv, g_kv)
    kr = rope(mkr[:, :, None, :], pos)[:, :, 0, :]
    qn = mqn.reshape(B, T, H_MLA, NOPE_DIM)
    qr = rope(mqr.reshape(B, T, H_MLA, ROPE_DIM), pos)
    q_cat = jnp.concatenate([jnp.einsum('bthn,rhn->bthr', qn, w_uk), qr], axis=-1)

    if past is None:
        k_all, v_all, logf_all, ckv_all, kr_all = fk, fv, logf, ckv, kr
    else:
        pk, pv, plogf, pckv, pkr = past
        k_all = jnp.concatenate([pk, fk], axis=1)
        v_all = jnp.concatenate([pv, fv], axis=1)
        logf_all = jnp.concatenate([plogf.astype(jnp.float32), logf], axis=1)
        ckv_all = jnp.concatenate([pckv, ckv], axis=1)
        kr_all = jnp.concatenate([pkr, kr], axis=1)
    c_all = jnp.cumsum(logf_all.astype(jnp.float32), axis=1)
    c_q = c_all[:, -T:]
    k_pos = jnp.arange(k_all.shape[1], dtype=jnp.int32)
    kv_cat = jnp.concatenate([ckv_all, kr_all], axis=-1)

    o_fox = sweep_queries(lambda q, cq, qp: fox_attention(q, k_all, v_all, cq, c_all, qp, k_pos), (fq, c_q), pos)
    o_lat = sweep_queries(lambda q, qp: mla_attention(q, kv_cat, ckv_all, qp, k_pos), (q_cat,), pos)
    o_mla = jnp.einsum('bthr,rhv->bthv', o_lat, w_uv)

    xp = jnp.concatenate([conv_buf.astype(lx.dtype), lx], axis=1)
    xc = conv_b + sum(xp[:, j:j + T] * conv_w[j] for j in range(CONV_W))
    new_conv = xp[:, -(CONV_W - 1):]
    xr = xc.reshape(B, T, N_LRU_BLOCKS, LRU_BLOCK)
    r = jax.nn.sigmoid((jnp.einsum('btnc,ncd->btnd', xr, w_gate_a).reshape(B, T, W_LRU) + b_gate_a).astype(jnp.float32))
    i = jax.nn.sigmoid((jnp.einsum('btnc,ncd->btnd', xr, w_gate_x).reshape(B, T, W_LRU) + b_gate_x).astype(jnp.float32))
    log_a = -LRU_C * r * jax.nn.softplus(-lru_lambda.astype(jnp.float32))
    a = jnp.exp(log_a)
    b = jnp.sqrt(-jnp.expm1(2.0 * log_a)) * (i * xc.astype(jnp.float32))
    b = b.at[:, 0].add(a[:, 0] * h0.astype(jnp.float32))
    _, h = lax.associative_scan(lru_combine, (a, b), axis=1)
    h_last = h[:, -1]

    mix = jnp.concatenate([o_fox.reshape(B, T, W_FOX) * jax.nn.silu(fg),
                           o_mla.reshape(B, T, W_MLA) * jax.nn.silu(mg),
                           h.astype(x.dtype) * jax.nn.silu(lg)], axis=-1)
    out = jnp.einsum('bte,ed->btd', mix, w_out)
    y = layernorm(DEEPNORM_ALPHA * x + out, ln_g, ln_b)
    return y, (fk, fv, logf, ckv, kr, h_last, new_conv)


def setup_inputs(seed: int = 0) -> dict:
    key = jax.random.key(seed)
    ks = jax.random.split(key, 24)
    n_pages = PAST_LEN // PAGE_SIZE
    n_used = DEC_BATCH * n_pages
    n_pool = n_used + n_used // 4
    nrm = lambda k, shape, scale=1.0: jax.random.normal(k, shape, jnp.float32) * scale
    x_prompt = nrm(ks[0], (BATCH, SEQ, D_MODEL))
    x_sample = nrm(ks[1], (DEC_BATCH, DEC_SEQ, D_MODEL))
    cache_fox_k = nrm(ks[2], (DEPTH, n_pool, PAGE_SIZE, H_FOX, HEAD_DIM))
    cache_fox_v = nrm(ks[3], (DEPTH, n_pool, PAGE_SIZE, H_FOX, HEAD_DIM))
    cache_fox_logf = jax.nn.log_sigmoid(FORGET_BIAS + nrm(ks[4], (DEPTH, n_pool, PAGE_SIZE, H_FOX)))
    cache_mla_ckv = nrm(ks[5], (DEPTH, n_pool, PAGE_SIZE, KV_LORA_RANK))
    cache_mla_krope = nrm(ks[6], (DEPTH, n_pool, PAGE_SIZE, ROPE_DIM))
    state_lru_h = nrm(ks[7], (DEPTH, DEC_BATCH, W_LRU), 0.5)
    state_lru_conv = nrm(ks[8], (DEPTH, DEC_BATCH, CONV_W - 1, W_LRU))
    page_table = jax.random.permutation(ks[9], n_pool)[:n_used].reshape(DEC_BATCH, n_pages).astype(jnp.int32)
    w_in = nrm(ks[10], (DEPTH, D_MODEL, D_IN), D_MODEL ** -0.5)
    b_f = FORGET_BIAS + nrm(ks[11], (DEPTH, H_FOX), 0.5)
    g_kv = 1.0 + nrm(ks[12], (DEPTH, KV_LORA_RANK), 0.02)
    w_uk = nrm(ks[13], (DEPTH, KV_LORA_RANK, H_MLA, NOPE_DIM), KV_LORA_RANK ** -0.5)
    w_uv = nrm(ks[14], (DEPTH, KV_LORA_RANK, H_MLA, V_DIM), KV_LORA_RANK ** -0.5)
    conv_w = nrm(ks[15], (DEPTH, CONV_W, W_LRU), CONV_W ** -0.5)
    conv_b = nrm(ks[16], (DEPTH, W_LRU), 0.01)
    w_gate_a = nrm(ks[17], (DEPTH, N_LRU_BLOCKS, LRU_BLOCK, LRU_BLOCK), LRU_BLOCK ** -0.5)
    b_gate_a = nrm(ks[18], (DEPTH, W_LRU), 0.01)
    w_gate_x = nrm(ks[19], (DEPTH, N_LRU_BLOCKS, LRU_BLOCK, LRU_BLOCK), LRU_BLOCK ** -0.5)
    b_gate_x = nrm(ks[20], (DEPTH, W_LRU), 0.01)
    a0 = jax.random.uniform(ks[21], (DEPTH, W_LRU), jnp.float32, 0.9, 0.999)
    lru_lambda = jnp.log(a0) - jnp.log1p(-a0)
    w_out = nrm(ks[22], (DEPTH, D_MIX, D_MODEL), D_MIX ** -0.5 * DEEPNORM_BETA)
    kln = jax.random.split(ks[23], 2)
    ln_g = 1.0 + nrm(kln[0], (DEPTH, D_MODEL), 0.02)
    ln_b = nrm(kln[1], (DEPTH, D_MODEL), 0.02)
    return {"x_prompt": x_prompt, "x_sample": x_sample,
            "cache_fox_k": cache_fox_k, "cache_fox_v": cache_fox_v, "cache_fox_logf": cache_fox_logf,
            "cache_mla_ckv": cache_mla_ckv, "cache_mla_krope": cache_mla_krope,
            "state_lru_h": state_lru_h, "state_lru_conv": state_lru_conv, "page_table": page_table,
            "w_in": w_in, "b_f": b_f, "g_kv": g_kv, "w_uk": w_uk, "w_uv": w_uv,
            "conv_w": conv_w, "conv_b": conv_b, "w_gate_a": w_gate_a, "b_gate_a": b_gate_a,
            "w_gate_x": w_gate_x, "b_gate_x": b_gate_x, "lru_lambda": lru_lambda,
            "w_out": w_out, "ln_g": ln_g, "ln_b": ln_b}


def reference(x_prompt, x_sample, cache_fox_k, cache_fox_v, cache_fox_logf, cache_mla_ckv, cache_mla_krope,
              state_lru_h, state_lru_conv, page_table, w_in, b_f, g_kv, w_uk, w_uv, conv_w, conv_b,
              w_gate_a, b_gate_a, w_gate_x, b_gate_x, lru_lambda, w_out, ln_g, ln_b):
    past_len = page_table.shape[1] * cache_fox_k.shape[2]
    pos_p = jnp.arange(x_prompt.shape[1], dtype=jnp.int32)
    pos_s = past_len + jnp.arange(x_sample.shape[1], dtype=jnp.int32)
    bp = x_prompt.shape[0]
    h0_p = jnp.zeros((bp, W_LRU), x_prompt.dtype)
    buf_p = jnp.zeros((bp, CONV_W - 1, W_LRU), x_prompt.dtype)
    xp, xs = x_prompt, x_sample
    p_states, s_states = [], []
    for l in range(DEPTH):
        lw = (w_in[l], b_f[l], g_kv[l], w_uk[l], w_uv[l], conv_w[l], conv_b[l], w_gate_a[l], b_gate_a[l],
              w_gate_x[l], b_gate_x[l], lru_lambda[l], w_out[l], ln_g[l], ln_b[l])
        xp, st_p = mixer_layer(xp, pos_p, None, h0_p, buf_p, *lw)
        past = (gather_pages(cache_fox_k[l], page_table), gather_pages(cache_fox_v[l], page_table),
                gather_pages(cache_fox_logf[l], page_table), gather_pages(cache_mla_ckv[l], page_table),
                gather_pages(cache_mla_krope[l], page_table))
        xs, st_s = mixer_layer(xs, pos_s, past, state_lru_h[l], state_lru_conv[l], *lw)
        p_states.append(st_p)
        s_states.append(st_s)
    p_fk, p_fv, p_logf, p_ckv, p_kr, p_h, p_conv = [jnp.stack(z) for z in zip(*p_states)]
    s_fk, s_fv, s_logf, s_ckv, s_kr, s_h, s_conv = [jnp.stack(z) for z in zip(*s_states)]
    return (xp, xs, p_fk, p_fv, p_logf, p_ckv, p_kr, p_h, p_conv, s_fk, s_fv, s_logf, s_ckv, s_kr, s_h, s_conv)
```

```python
import functools
import math

import jax
import jax.numpy as jnp
from jax import lax
from jax.experimental import pallas as pl
from jax.experimental.pallas import tpu as pltpu

F32 = jnp.float32
BF16 = jnp.bfloat16

HEAD_DIM = 64
N_HEADS = 4
W_ATT = N_HEADS * HEAD_DIM
ROPE_DIM = 32
LORA = 128
W_LRU = 512
CONV_W = 4
LRU_C = 8.0
ROPE_BASE = 10000.0
LN_EPS = 1e-5
RMS_EPS = 1e-6
NEG_INF = -1e30
MLA_SCALE = 1.0 / math.sqrt(HEAD_DIM + ROPE_DIM)
FOX_SCALE = 1.0 / math.sqrt(HEAD_DIM)

LANES = 128
SUBLANES = 8
VMEM_LIMIT = 56 * 1024 * 1024

C_FQ, C_FK, C_FV, C_FG = 0, 256, 512, 768
C_QN, C_QR, C_CKV, C_KR, C_MG = 1024, 1280, 1408, 1536, 1664
C_LX, C_LG, C_FF = 1920, 2432, 2944
NZ = 3072

TM_FRONT = 512
TQ = 256
TK = 512
PAGES_PER_CHUNK = 16
N_KEYS_NEW = 16


def _mm(a, b):
    return jnp.dot(a, b, preferred_element_type=F32)


def _mm_nt(a, b):
    return lax.dot_general(a, b, (((1,), (1,)), ((), ())), preferred_element_type=F32)


def _mm_exact(a, b):
    return jnp.dot(a, b, preferred_element_type=F32, precision=lax.Precision.HIGHEST)


def _rope(x, cos, sin_a, sin_b):
    return x * cos + pltpu.roll(x, LANES - ROPE_DIM // 2, 1) * sin_a + pltpu.roll(x, ROPE_DIM // 2, 1) * sin_b


def _silu(x):
    return x * jax.nn.sigmoid(x)


def _softplus(x):
    return jnp.maximum(x, 0.0) + jnp.log1p(jnp.exp(-jnp.abs(x)))


def _front_common(x_ref, w_ref, bf_ref, gkv_ref, cos_ref, sa_ref, sb_ref, wuk_ref,
                  qf_ref, kf_ref, kfb_ref, vf_ref, vfb_ref, gf_ref, qlat_ref, qr_ref,
                  ckv_ref, kr_ref, kvc_ref, gm_ref, logf_ref):
    xb = x_ref[...].astype(BF16)

    def seg(start, width):
        return _mm(xb, w_ref[:, start:start + width])

    qf_ref[...] = (seg(C_FQ, W_ATT) * FOX_SCALE).astype(BF16)
    fk = seg(C_FK, W_ATT)
    kf_ref[...] = fk
    kfb_ref[...] = fk.astype(BF16)
    fv = seg(C_FV, W_ATT)
    vf_ref[...] = fv
    vfb_ref[...] = fv.astype(BF16)
    gf_ref[...] = _silu(seg(C_FG, W_ATT))

    qn = seg(C_QN, W_ATT).astype(BF16)
    qlat_ref[...] = _mm(qn, wuk_ref[...]).astype(BF16)
    cos, sa, sb = cos_ref[...], sa_ref[...], sb_ref[...]
    qr_ref[...] = _rope(seg(C_QR, LANES), cos, sa, sb).astype(BF16)

    mckv = seg(C_CKV, LORA)
    ms = jnp.mean(mckv * mckv, axis=-1, keepdims=True)
    ckv = mckv * lax.rsqrt(ms + RMS_EPS) * gkv_ref[...]
    ckv_ref[...] = ckv
    kvc_ref[:, 0:LORA] = ckv.astype(BF16)
    krt = _rope(seg(C_KR, LANES), cos, sa, sb)
    kr_ref[...] = krt[:, 0:ROPE_DIM]
    kvc_ref[:, LORA:2 * LORA] = krt.astype(BF16)
    gm_ref[...] = _silu(seg(C_MG, W_ATT))

    y = seg(C_FF, LANES) + bf_ref[...]
    ls = jnp.minimum(y, 0.0) - jnp.log1p(jnp.exp(-jnp.abs(y)))
    logf_ref[...] = ls[:, 0:N_HEADS]

    return seg(C_LX, W_LRU), seg(C_LG, W_LRU), ls


def _lru_coeffs(xc, wga_ref, wgx_ref, bga_ref, bgx_ref, lam_ref):
    xcb = xc.astype(BF16)
    half = W_LRU // 2

    def gate(w_ref, b_ref):
        g = jnp.concatenate([_mm(xcb[:, :half], w_ref[0]), _mm(xcb[:, half:], w_ref[1])], axis=-1)
        return jax.nn.sigmoid(g + b_ref[...])

    r = gate(wga_ref, bga_ref)
    i = gate(wgx_ref, bgx_ref)
    log_a = -LRU_C * r * _softplus(-lam_ref[...])
    a = jnp.exp(log_a)
    b = jnp.sqrt(1.0 - jnp.exp(2.0 * log_a)) * (i * xc)
    return a, b


def _front_prompt_kernel(x_ref, w_ref, bf_ref, gkv_ref, cos_ref, sa_ref, sb_ref, wuk_ref,
                         cw_ref, cb_ref, wga_ref, wgx_ref, bga_ref, bgx_ref, lam_ref, tri_ref,
                         qf_ref, kf_ref, kfb_ref, vf_ref, vfb_ref, gf_ref, qlat_ref, qr_ref,
                         ckv_ref, kr_ref, kvc_ref, gm_ref, logf_ref,
                         c_ref, hl_ref, hlast_ref, nconv_ref,
                         xpad_scr, a_scr, b_scr, h_scr, hcar_scr, ccar_scr, *, tm):
    i = pl.program_id(1)

    @pl.when(i == 0)
    def _():
        xpad_scr[0:SUBLANES, :] = jnp.zeros((SUBLANES, W_LRU), F32)
        hcar_scr[...] = jnp.zeros_like(hcar_scr)
        ccar_scr[...] = jnp.zeros_like(ccar_scr)

    lx, lg, ls = _front_common(x_ref, w_ref, bf_ref, gkv_ref, cos_ref, sa_ref, sb_ref, wuk_ref,
                               qf_ref, kf_ref, kfb_ref, vf_ref, vfb_ref, gf_ref, qlat_ref, qr_ref,
                               ckv_ref, kr_ref, kvc_ref, gm_ref, logf_ref)

    c = _mm_exact(tri_ref[...], ls) + ccar_scr[...]
    c_ref[...] = c[:, 0:N_HEADS]
    ccar_scr[...] = c[tm - 1:tm, :]

    xpad_scr[SUBLANES:SUBLANES + tm, :] = lx
    xc = cb_ref[...] + xpad_scr[SUBLANES:SUBLANES + tm, :] * cw_ref[CONV_W - 1:CONV_W, :]
    for j in range(1, CONV_W):
        xc = xc + xpad_scr[SUBLANES - j:SUBLANES - j + tm, :] * cw_ref[CONV_W - 1 - j:CONV_W - j, :]
    tail = xpad_scr[tm:tm + SUBLANES, :]
    xpad_scr[0:SUBLANES, :] = tail
    nconv_ref[...] = tail[SUBLANES - (CONV_W - 1):, :]

    a, b = _lru_coeffs(xc, wga_ref, wgx_ref, bga_ref, bgx_ref, lam_ref)
    a_scr[...] = a
    b_scr[...] = b

    sub = lax.broadcasted_iota(jnp.int32, (SUBLANES, W_LRU), 0)

    def group(g, carry):
        r0 = pl.multiple_of(g * SUBLANES, SUBLANES)
        a8 = a_scr[pl.ds(r0, SUBLANES), :]
        b8 = b_scr[pl.ds(r0, SUBLANES), :]
        for d in (1, 2, 4):
            keep = sub >= d
            b8 = jnp.where(keep, a8 * pltpu.roll(b8, d, 0) + b8, b8)
            a8 = jnp.where(keep, a8 * pltpu.roll(a8, d, 0), a8)
        h8 = a8 * carry + b8
        h_scr[pl.ds(r0, SUBLANES), :] = h8
        return h8[SUBLANES - 1:SUBLANES, :]

    carry = lax.fori_loop(0, tm // SUBLANES, group, hcar_scr[...], unroll=4)
    hcar_scr[...] = carry
    hlast_ref[...] = carry
    hl_ref[...] = h_scr[...] * _silu(lg)


def _front_sample_kernel(x_ref, w_ref, bf_ref, gkv_ref, cos_ref, sa_ref, sb_ref, wuk_ref,
                         cw_ref, cb_ref, wga_ref, wgx_ref, bga_ref, bgx_ref, lam_ref, h0_ref, cbuf_ref,
                         qf_ref, kf_ref, kfb_ref, vf_ref, vfb_ref, gf_ref, qlat_ref, qr_ref,
                         ckv_ref, kr_ref, kvc_ref, gm_ref, logf_ref,
                         cn_ref, hl_ref, hlast_ref, nconv_ref, *, nb, nt):
    lx, lg, ls = _front_common(x_ref, w_ref, bf_ref, gkv_ref, cos_ref, sa_ref, sb_ref, wuk_ref,
                               qf_ref, kf_ref, kfb_ref, vf_ref, vfb_ref, gf_ref, qlat_ref, qr_ref,
                               ckv_ref, kr_ref, kvc_ref, gm_ref, logf_ref)
    xs = [cbuf_ref[j] for j in range(CONV_W - 1)] + [lx[t * nb:(t + 1) * nb, :] for t in range(nt)]
    xc = jnp.concatenate(
        [cb_ref[...] + sum(xs[t + j] * cw_ref[j:j + 1, :] for j in range(CONV_W)) for t in range(nt)], axis=0)
    for j in range(CONV_W - 1):
        nconv_ref[j] = xs[nt + j]

    a, b = _lru_coeffs(xc, wga_ref, wgx_ref, bga_ref, bgx_ref, lam_ref)
    h = h0_ref[...]
    cn = jnp.zeros((nb, N_HEADS), F32)
    hs = []
    for t in range(nt):
        h = a[t * nb:(t + 1) * nb, :] * h + b[t * nb:(t + 1) * nb, :]
        hs.append(h)
        cn = cn + ls[t * nb:(t + 1) * nb, 0:N_HEADS]
        cn_ref[t * nb:(t + 1) * nb, :] = cn
    hlast_ref[...] = h
    hl_ref[...] = jnp.concatenate(hs, axis=0) * _silu(lg)


def _row_spec(tm, c, lead=True):
    if lead:
        return pl.BlockSpec((None, tm, c), lambda b, i: (b, i, 0))
    return pl.BlockSpec((tm, c), lambda i: (i, 0))


def _const_spec(shape, ngrid):
    zeros = (0,) * len(shape)
    if ngrid == 2:
        return pl.BlockSpec(shape, lambda b, i: zeros)
    return pl.BlockSpec(shape, lambda i: zeros)


_FRONT_OUT_COLS = (
    ("qf", W_ATT, BF16), ("kf", W_ATT, F32), ("kfb", W_ATT, BF16), ("vf", W_ATT, F32), ("vfb", W_ATT, BF16),
    ("gf", W_ATT, F32), ("qlat", N_HEADS * LORA, BF16), ("qr", LANES, BF16), ("ckv", LORA, F32),
    ("kr", ROPE_DIM, F32), ("kvc", 2 * LORA, BF16), ("gm", W_ATT, F32), ("logf", N_HEADS, F32),
    ("c", N_HEADS, F32), ("hl", W_LRU, F32))


def _front_prompt(x, lw, tables, tri):
    nb, t, d = x.shape
    tm = TM_FRONT
    consts = [lw["w_in"], lw["b_f"], lw["g_kv"]]
    in_specs = [_row_spec(tm, d), _const_spec(lw["w_in"].shape, 2), _const_spec((1, LANES), 2),
                _const_spec((1, LANES), 2)]
    in_specs += [pl.BlockSpec((tm, LANES), lambda b, i: (i, 0))] * 3
    rest = [lw["wuk_bd"], lw["conv_w"], lw["conv_b"], lw["wga"], lw["wgx"], lw["b_gate_a"], lw["b_gate_x"],
            lw["lam"], tri]
    in_specs += [_const_spec(r.shape, 2) for r in rest]
    out_shape = [jax.ShapeDtypeStruct((nb, t, c), dt) for _, c, dt in _FRONT_OUT_COLS]
    out_specs = [_row_spec(tm, c) for _, c, _ in _FRONT_OUT_COLS]
    out_shape += [jax.ShapeDtypeStruct((nb, 1, W_LRU), F32), jax.ShapeDtypeStruct((nb, CONV_W - 1, W_LRU), F32)]
    out_specs += [pl.BlockSpec((None, 1, W_LRU), lambda b, i: (b, 0, 0)),
                  pl.BlockSpec((None, CONV_W - 1, W_LRU), lambda b, i: (b, 0, 0))]
    outs = pl.pallas_call(
        functools.partial(_front_prompt_kernel, tm=tm),
        grid=(nb, t // tm),
        in_specs=in_specs, out_specs=out_specs, out_shape=out_shape,
        scratch_shapes=[pltpu.VMEM((tm + SUBLANES, W_LRU), F32), pltpu.VMEM((tm, W_LRU), F32),
                        pltpu.VMEM((tm, W_LRU), F32), pltpu.VMEM((tm, W_LRU), F32),
                        pltpu.VMEM((1, W_LRU), F32), pltpu.VMEM((1, LANES), F32)],
        compiler_params=pltpu.CompilerParams(dimension_semantics=("arbitrary", "arbitrary"),
                                             vmem_limit_bytes=VMEM_LIMIT),
        name="front_prompt",
    )(x, *consts, *tables, *rest)
    names = [n for n, _, _ in _FRONT_OUT_COLS] + ["hlast", "nconv"]
    return dict(zip(names, outs))


def _front_sample(x, lw, tables, h0, cbuf, nb, nt):
    rows, d = x.shape
    args = [x, lw["w_in"], lw["b_f"], lw["g_kv"], *tables, lw["wuk_bd"], lw["conv_w"], lw["conv_b"], lw["wga"],
            lw["wgx"], lw["b_gate_a"], lw["b_gate_x"], lw["lam"], h0, cbuf]
    in_specs = [_const_spec(a.shape, 1) for a in args]
    out_shape = [jax.ShapeDtypeStruct((rows, c), dt) for _, c, dt in _FRONT_OUT_COLS]
    out_shape += [jax.ShapeDtypeStruct((nb, W_LRU), F32), jax.ShapeDtypeStruct((CONV_W - 1, nb, W_LRU), F32)]
    out_specs = [_const_spec(s.shape, 1) for s in out_shape]
    outs = pl.pallas_call(
        functools.partial(_front_sample_kernel, nb=nb, nt=nt),
        grid=(1,),
        in_specs=in_specs, out_specs=out_specs, out_shape=out_shape,
        compiler_params=pltpu.CompilerParams(dimension_semantics=("arbitrary",), vmem_limit_bytes=VMEM_LIMIT),
        name="front_sample",
    )(*args)
    names = [n for n, _, _ in _FRONT_OUT_COLS] + ["hlast", "nconv"]
    return dict(zip(names, outs))


def _online_update(carry, t, cq, pv_fn):
    m, l, acc = carry
    row_max = jnp.max(t, axis=-1, keepdims=True)
    m_new = jnp.maximum(m, row_max if cq is None else row_max + cq)
    shift = -m_new if cq is None else cq - m_new
    p = jnp.exp(t + shift)
    alpha = jnp.exp(m - m_new)
    l_new = alpha * l + jnp.sum(p, axis=-1, keepdims=True)
    acc_new = alpha * acc + pv_fn(p.astype(BF16))
    return m_new, l_new, acc_new


def _attn_prompt_kernel(qf_ref, kfb_ref, vfb_ref, cq_ref, ckt_ref, qlat_ref, qr_ref, kvc_ref,
                        of_ref, ol_ref, *, tq, tk):
    i = pl.program_id(1)
    n_full = (i * tq) // tk
    diag_off = i * tq - n_full * tk
    lane = lax.broadcasted_iota(jnp.int32, (tq, LANES), 1)

    def init(rows):
        return (jnp.full((rows, 1), NEG_INF, F32), jnp.zeros((rows, 1), F32), jnp.zeros((rows, LANES), F32))

    d_fox = (lax.broadcasted_iota(jnp.int32, (tq, tk), 1) - lax.broadcasted_iota(jnp.int32, (tq, tk), 0))
    for pair in range(N_HEADS // 2):
        cols = slice(pair * LANES, (pair + 1) * LANES)
        qp = qf_ref[:, cols]
        o_pair = []
        for hh in range(2):
            h = 2 * pair + hh
            in_head = (lane >= HEAD_DIM) if hh else (lane < HEAD_DIM)
            qh = jnp.where(in_head, qp, jnp.zeros_like(qp))
            cqh = cq_ref[:, h:h + 1]

            def step(j, carry, masked, qh=qh, cqh=cqh, h=h, cols=cols):
                off = pl.multiple_of(j * tk, tk)
                kt = kfb_ref[pl.ds(off, tk), cols]
                vt = vfb_ref[pl.ds(off, tk), cols]
                t = _mm_nt(qh, kt) - ckt_ref[j][h:h + 1, :]
                if masked:
                    t = jnp.where(d_fox <= diag_off, t, NEG_INF)
                return _online_update(carry, t, cqh, lambda p: _mm(p, vt))

            carry = lax.fori_loop(0, n_full, lambda j, c: step(j, c, False), init(tq))
            _, l, acc = step(n_full, carry, True)
            o_pair.append(acc * (1.0 / l))
        of_ref[:, cols] = jnp.where(lane < HEAD_DIM, o_pair[0], o_pair[1])

    qr = qr_ref[...]
    qs = jnp.concatenate(
        [jnp.concatenate([qlat_ref[:, h * LORA:(h + 1) * LORA],
                          jnp.where(lane // ROPE_DIM == h, qr, jnp.zeros_like(qr))], axis=-1)
         for h in range(N_HEADS)], axis=0)
    rows = N_HEADS * tq
    d_mla = (lax.broadcasted_iota(jnp.int32, (rows, tk), 1)
             - (lax.broadcasted_iota(jnp.int32, (rows, tk), 0) & (tq - 1)))

    def mstep(j, carry, masked):
        off = pl.multiple_of(j * tk, tk)
        kv = kvc_ref[pl.ds(off, tk), :]
        t = _mm_nt(qs, kv) * MLA_SCALE
        if masked:
            t = jnp.where(d_mla <= diag_off, t, NEG_INF)
        return _online_update(carry, t, None, lambda p: _mm(p, kv[:, 0:LORA]))

    carry = lax.fori_loop(0, n_full, lambda j, c: mstep(j, c, False), init(rows))
    _, l, acc = mstep(n_full, carry, True)
    o = acc * (1.0 / l)
    for h in range(N_HEADS):
        ol_ref[:, h * LORA:(h + 1) * LORA] = o[h * tq:(h + 1) * tq, :]


def _attn_prompt(f, ckt):
    nb, t, _ = f["qf"].shape
    tq, tk = TQ, TK
    assert tk % tq == 0 and tq & (tq - 1) == 0 and t % tk == 0
    qspec = lambda c: pl.BlockSpec((None, tq, c), lambda b, i: (b, i, 0))
    fullspec = lambda c: pl.BlockSpec((None, t, c), lambda b, i: (b, 0, 0))
    return pl.pallas_call(
        functools.partial(_attn_prompt_kernel, tq=tq, tk=tk),
        grid=(nb, t // tq),
        in_specs=[qspec(W_ATT), fullspec(W_ATT), fullspec(W_ATT), qspec(N_HEADS),
                  pl.BlockSpec((None, t // tk, N_HEADS, tk), lambda b, i: (b, 0, 0, 0)),
                  qspec(N_HEADS * LORA), qspec(LANES), fullspec(2 * LORA)],
        out_specs=[qspec(W_ATT), qspec(N_HEADS * LORA)],
        out_shape=[jax.ShapeDtypeStruct((nb, t, W_ATT), F32), jax.ShapeDtypeStruct((nb, t, N_HEADS * LORA), F32)],
        compiler_params=pltpu.CompilerParams(dimension_semantics=("arbitrary", "arbitrary"),
                                             vmem_limit_bytes=VMEM_LIMIT),
        name="attn_prompt",
    )(f["qf"], f["kfb"], f["vfb"], f["c"], ckt, f["qlat"], f["qr"], f["kvc"])


def _attn_decode_kernel(pt_ref, qf_ref, qm_ref, cqn_ref, cnr_ref, kn_ref, vn_ref, kvn_ref, ucum_ref, mlow_ref,
                        ck_hbm, cv_hbm, clf_hbm, cckv_hbm, ckr_hbm,
                        of_ref, ol_ref,
                        kbuf, vbuf, cbuf, rbuf, lfbuf, cp_scr, sem, lsem,
                        *, layer, nb, n_pages, page, nq):
    b = pl.program_id(0)
    ppc = PAGES_PER_CHUNK
    n_chunks = n_pages // ppc
    ck_keys = ppc * page
    rows = N_HEADS * nq

    def chunk_copies(bb, c, slot):
        cps = []
        for p in range(ppc):
            pg = pt_ref[bb, c * ppc + p]
            dst = pl.ds(p * page, page)
            cps.append(pltpu.make_async_copy(ck_hbm.at[layer, pg], kbuf.at[slot, :, dst], sem.at[0, slot]))
            cps.append(pltpu.make_async_copy(cv_hbm.at[layer, pg], vbuf.at[slot, :, dst], sem.at[1, slot]))
            cps.append(pltpu.make_async_copy(cckv_hbm.at[layer, pg], cbuf.at[slot, dst, :], sem.at[2, slot]))
            cps.append(pltpu.make_async_copy(ckr_hbm.at[layer, pg], rbuf.at[slot, :, dst], sem.at[3, slot]))
        return cps

    def logf_copies(bb, slot):
        return [pltpu.make_async_copy(clf_hbm.at[layer, pt_ref[bb, p]], lfbuf.at[slot, :, pl.ds(p, 1), :],
                                      lsem.at[slot]) for p in range(n_pages)]

    lslot = b % 2

    @pl.when(b == 0)
    def _():
        for cp in logf_copies(0, 0):
            cp.start()
        for cp in chunk_copies(0, 0, 0):
            cp.start()

    @pl.when(b + 1 < nb)
    def _():
        for cp in logf_copies(b + 1, 1 - lslot):
            cp.start()

    for cp in logf_copies(b, lslot):
        cp.wait()

    lf = lfbuf[lslot].reshape(N_HEADS * n_pages, page)
    within = _mm_exact(lf, ucum_ref[...])
    tot = jnp.broadcast_to(within[:, page - 1:page], within.shape)
    cpast = within + _mm_exact(mlow_ref[...], tot)
    cp_scr[...] = cpast
    ctot = jnp.concatenate(
        [jnp.broadcast_to(cpast[(h + 1) * n_pages - 1:(h + 1) * n_pages, page - 1:page], (nq, 1))
         for h in range(N_HEADS)], axis=0)
    cq = ctot + cqn_ref[...]

    qf = qf_ref[...]
    qm = qm_ref[...]
    fox = (jnp.full((rows, 1), NEG_INF, F32), jnp.zeros((rows, 1), F32), jnp.zeros((rows, W_ATT), F32))
    mla = (jnp.full((rows, 1), NEG_INF, F32), jnp.zeros((rows, 1), F32), jnp.zeros((rows, LORA), F32))

    for c in range(n_chunks):
        slot = c % 2
        if c + 1 < n_chunks:
            for cp in chunk_copies(b, c + 1, 1 - slot):
                cp.start()
        else:
            @pl.when(b + 1 < nb)
            def _():
                for cp in chunk_copies(b + 1, 0, 1 - slot):
                    cp.start()
        for cp in chunk_copies(b, c, slot):
            cp.wait()

        kt = kbuf[slot].astype(BF16)
        vt = vbuf[slot].astype(BF16)
        bias = jnp.concatenate(
            [jnp.concatenate([jnp.broadcast_to(cp_scr[h * n_pages + c * ppc + p:h * n_pages + c * ppc + p + 1, :],
                                               (nq, page)) for h in range(N_HEADS)], axis=0)
             for p in range(ppc)], axis=1)
        fox = _online_update(fox, _mm(qf, kt) - bias, cq, lambda p: _mm_nt(p, vt))

        cb = cbuf[slot].astype(BF16)
        rt = rbuf[slot].astype(BF16)
        s = (_mm_nt(qm[:, 0:LORA], cb) + _mm(qm[:, LORA:LORA + ROPE_DIM], rt)) * MLA_SCALE
        mla = _online_update(mla, s, None, lambda p: _mm(p, cb))

    rq = lax.broadcasted_iota(jnp.int32, (rows, N_KEYS_NEW), 0) & (nq - 1)
    visible = lax.broadcasted_iota(jnp.int32, (rows, N_KEYS_NEW), 1) <= rq
    knb = kn_ref[...].astype(BF16)
    vnb = vn_ref[...].astype(BF16)
    cnew = jnp.concatenate([jnp.broadcast_to(cnr_ref[h:h + 1, :], (nq, N_KEYS_NEW)) for h in range(N_HEADS)], axis=0)
    t = jnp.where(visible, _mm_nt(qf, knb) - (ctot + cnew), NEG_INF)
    _, l, acc = _online_update(fox, t, cq, lambda p: _mm(p, vnb))
    o = acc * (1.0 / l)
    lane = lax.broadcasted_iota(jnp.int32, (nq, W_ATT), 1)
    of = jnp.zeros((nq, W_ATT), F32)
    for h in range(N_HEADS):
        of = jnp.where(lane // HEAD_DIM == h, o[h * nq:(h + 1) * nq, :], of)
    of_ref[...] = of

    kvn = kvn_ref[...].astype(BF16)
    t = jnp.where(visible, _mm_nt(qm, kvn) * MLA_SCALE, NEG_INF)
    _, l, acc = _online_update(mla, t, None, lambda p: _mm(p, kvn[:, 0:LORA]))
    ol_ref[...] = acc * (1.0 / l)


def _attn_decode(layer, page_table, qf_bd, qm, cqn, cnr, kn, vn, kvn, ucum, mlow, caches):
    nb, n_pages = page_table.shape
    ck, cv, clf, cckv, ckr = caches
    page = ck.shape[3]
    rows = qf_bd.shape[1]
    nq = rows // N_HEADS
    ck_keys = PAGES_PER_CHUNK * page
    assert n_pages % (2 * PAGES_PER_CHUNK) == 0 and nq & (nq - 1) == 0 and n_pages % SUBLANES == 0
    bspec = lambda r, c: pl.BlockSpec((None, r, c), lambda b, pt: (b, 0, 0))
    cspec = lambda s: pl.BlockSpec(s, lambda b, pt: (0,) * len(s))
    anyspec = pl.BlockSpec(memory_space=pl.ANY)
    grid_spec = pltpu.PrefetchScalarGridSpec(
        num_scalar_prefetch=1,
        grid=(nb,),
        in_specs=[bspec(rows, W_ATT), bspec(rows, 2 * LORA), bspec(rows, 1), bspec(N_HEADS, N_KEYS_NEW),
                  bspec(N_KEYS_NEW, W_ATT), bspec(N_KEYS_NEW, W_ATT), bspec(N_KEYS_NEW, 2 * LORA),
                  cspec(ucum.shape), cspec(mlow.shape), anyspec, anyspec, anyspec, anyspec, anyspec],
        out_specs=[bspec(nq, W_ATT), bspec(rows, LORA)],
        scratch_shapes=[pltpu.VMEM((2, W_ATT, ck_keys), F32), pltpu.VMEM((2, W_ATT, ck_keys), F32),
                        pltpu.VMEM((2, ck_keys, LORA), F32), pltpu.VMEM((2, ROPE_DIM, ck_keys), F32),
                        pltpu.VMEM((2, N_HEADS, n_pages, page), F32),
                        pltpu.VMEM((N_HEADS * n_pages, page), F32),
                        pltpu.SemaphoreType.DMA((4, 2)), pltpu.SemaphoreType.DMA((2,))])
    return pl.pallas_call(
        functools.partial(_attn_decode_kernel, layer=layer, nb=nb, n_pages=n_pages, page=page, nq=nq),
        grid_spec=grid_spec,
        out_shape=[jax.ShapeDtypeStruct((nb, nq, W_ATT), F32), jax.ShapeDtypeStruct((nb, rows, LORA), F32)],
        compiler_params=pltpu.CompilerParams(dimension_semantics=("arbitrary",), vmem_limit_bytes=VMEM_LIMIT),
        name="attn_decode",
    )(page_table, qf_bd, qm, cqn, cnr, kn, vn, kvn, ucum, mlow, ck, cv, clf, cckv, ckr)


def _back_kernel(x_ref, of_ref, ol_ref, gf_ref, gm_ref, hl_ref, wuv_ref, wout_ref, g_ref, b_ref, y_ref, *, alpha):
    o_mla = _mm(ol_ref[...].astype(BF16), wuv_ref[...])
    mix = jnp.concatenate([of_ref[...] * gf_ref[...], o_mla * gm_ref[...], hl_ref[...]], axis=-1).astype(BF16)
    r = alpha * x_ref[...] + _mm(mix, wout_ref[...])
    mu = jnp.mean(r, axis=-1, keepdims=True)
    rc = r - mu
    var = jnp.mean(rc * rc, axis=-1, keepdims=True)
    y_ref[...] = rc * lax.rsqrt(var + LN_EPS) * g_ref[...] + b_ref[...]


def _back(x, of, ol, gf, gm, hl, lw, alpha, tm):
    rows, d = x.shape
    rspec = lambda c: pl.BlockSpec((tm, c), lambda i: (i, 0))
    consts = [lw["wuv_bd"], lw["w_out"], lw["ln_g"], lw["ln_b"]]
    return pl.pallas_call(
        functools.partial(_back_kernel, alpha=alpha),
        grid=(rows // tm,),
        in_specs=[rspec(d), rspec(W_ATT), rspec(N_HEADS * LORA), rspec(W_ATT), rspec(W_ATT), rspec(W_LRU)]
        + [_const_spec(c.shape, 1) for c in consts],
        out_specs=rspec(d),
        out_shape=jax.ShapeDtypeStruct((rows, d), F32),
        compiler_params=pltpu.CompilerParams(dimension_semantics=("arbitrary",), vmem_limit_bytes=VMEM_LIMIT),
        name="back",
    )(x, of, ol, gf, gm, hl, *consts)


def _rope_tables(pos):
    half = ROPE_DIM // 2
    lane = jnp.arange(LANES)
    inv = ROPE_BASE ** (-jnp.arange(half, dtype=F32) / half)
    ang = pos.astype(F32)[:, None] * inv[lane % half][None, :]
    first = ((lane % ROPE_DIM) < half)[None, :]
    sin = jnp.sin(ang)
    return jnp.cos(ang), jnp.where(first, -sin, 0.0), jnp.where(first, 0.0, sin)


def _block_diag(blocks):
    n, r, c = blocks.shape
    eye = jnp.eye(n, dtype=blocks.dtype)
    return (eye[:, None, :, None] * blocks[:, :, None, :]).reshape(n * r, n * c)


def _prep_layer(l, w_in, b_f, g_kv, w_uk, w_uv, conv_w, conv_b, w_gate_a, b_gate_a, w_gate_x, b_gate_x,
                lru_lambda, w_out, ln_g, ln_b):
    wi = w_in[l]
    o = 0
    parts = {}
    for name, width in (("fq", 256), ("fk", 256), ("fv", 256), ("ff", 4), ("fg", 256), ("qn", 256), ("qr", 128),
                        ("ckv", 128), ("kr", 32), ("mg", 256), ("lx", 512), ("lg", 512)):
        parts[name] = wi[:, o:o + width]
        o += width
    d = wi.shape[0]
    w_cat = jnp.concatenate(
        [parts["fq"], parts["fk"], parts["fv"], parts["fg"], parts["qn"], parts["qr"], parts["ckv"],
         parts["kr"], parts["kr"], parts["kr"], parts["kr"], parts["mg"], parts["lx"], parts["lg"],
         parts["ff"], jnp.zeros((d, LANES - N_HEADS), wi.dtype)], axis=1).astype(BF16)
    assert w_cat.shape[1] == NZ
    row = lambda v: v.reshape(1, -1).astype(F32)
    nblk = w_gate_a.shape[1]
    gate_bd = lambda w: jnp.stack([_block_diag(w[:nblk // 2]), _block_diag(w[nblk // 2:])]).astype(BF16)
    return {
        "w_in": w_cat,
        "b_f": jnp.pad(row(b_f[l]), ((0, 0), (0, LANES - N_HEADS))),
        "g_kv": row(g_kv[l]),
        "wuk_bd": _block_diag(jnp.transpose(w_uk[l], (1, 2, 0))).astype(BF16),
        "wuv_bd": _block_diag(jnp.transpose(w_uv[l], (1, 0, 2))).astype(BF16),
        "conv_w": conv_w[l].astype(F32), "conv_b": row(conv_b[l]),
        "wga": gate_bd(w_gate_a[l]), "wgx": gate_bd(w_gate_x[l]),
        "b_gate_a": row(b_gate_a[l]), "b_gate_x": row(b_gate_x[l]), "lam": row(lru_lambda[l]),
        "w_out": w_out[l].astype(BF16), "ln_g": row(ln_g[l]), "ln_b": row(ln_b[l]),
    }


def kernel(x_prompt, x_sample, cache_fox_k, cache_fox_v, cache_fox_logf, cache_mla_ckv, cache_mla_krope,
           state_lru_h, state_lru_conv, page_table, w_in, b_f, g_kv, w_uk, w_uv, conv_w, conv_b,
           w_gate_a, b_gate_a, w_gate_x, b_gate_x, lru_lambda, w_out, ln_g, ln_b):
    depth = w_in.shape[0]
    nbp, t_p, d = x_prompt.shape
    nbs, t_s, _ = x_sample.shape
    n_pool, page = cache_fox_k.shape[1], cache_fox_k.shape[2]
    n_pages = page_table.shape[1]
    past_len = n_pages * page
    alpha = (2 * depth) ** 0.25
    assert t_s == SUBLANES and t_s <= N_KEYS_NEW

    tables_p = _rope_tables(jnp.arange(t_p, dtype=jnp.int32))
    tables_s = _rope_tables(past_len + jnp.repeat(jnp.arange(t_s, dtype=jnp.int32), nbs))
    tri = jnp.tril(jnp.ones((TM_FRONT, TM_FRONT), F32))
    ucum = jnp.triu(jnp.ones((page, page), F32))
    idx = jnp.arange(N_HEADS * n_pages)
    mlow = ((idx[:, None] // n_pages == idx[None, :] // n_pages) & (idx[None, :] < idx[:, None])).astype(F32)

    feat_major = lambda a: jnp.transpose(a, (0, 1, 3, 4, 2)).reshape(depth, n_pool, W_ATT, page)
    caches = (feat_major(cache_fox_k), feat_major(cache_fox_v),
              jnp.swapaxes(cache_fox_logf, 2, 3).reshape(depth, n_pool, N_HEADS, 1, page),
              cache_mla_ckv, jnp.swapaxes(cache_mla_krope, 2, 3))

    tm_major = lambda a: jnp.swapaxes(a, 0, 1).reshape(t_s * nbs, *a.shape[2:])
    b_major = lambda a: jnp.swapaxes(a.reshape(t_s, nbs, *a.shape[1:]), 0, 1)
    head_rows = lambda a: jnp.swapaxes(a, 1, 2).reshape(nbs, N_HEADS * t_s, a.shape[-1])
    pad_keys = lambda a: jnp.pad(a, ((0, 0), (0, N_KEYS_NEW - t_s), (0, 0)))
    head_of_lane = jnp.arange(W_ATT) // HEAD_DIM

    xp = x_prompt
    xs = tm_major(x_sample)
    p_states, s_states = [], []
    for l in range(depth):
        lw = _prep_layer(l, w_in, b_f, g_kv, w_uk, w_uv, conv_w, conv_b, w_gate_a, b_gate_a, w_gate_x, b_gate_x,
                         lru_lambda, w_out, ln_g, ln_b)

        f = _front_prompt(xp, lw, tables_p, tri)
        ckt = jnp.swapaxes(f["c"].reshape(nbp, t_p // TK, TK, N_HEADS), 2, 3)
        of, ol = _attn_prompt(f, ckt)
        flat = lambda a: a.reshape(nbp * t_p, a.shape[-1])
        xp = _back(flat(xp), flat(of), flat(ol), flat(f["gf"]), flat(f["gm"]), flat(f["hl"]), lw, alpha,
                   TM_FRONT).reshape(nbp, t_p, d)
        p_states.append((f["kf"].reshape(nbp, t_p, N_HEADS, HEAD_DIM), f["vf"].reshape(nbp, t_p, N_HEADS, HEAD_DIM),
                         f["logf"], f["ckv"], f["kr"], f["hlast"].reshape(nbp, W_LRU), f["nconv"]))

        g = _front_sample(xs, lw, tables_s, state_lru_h[l], jnp.swapaxes(state_lru_conv[l], 0, 1), nbs, t_s)
        kf, vf, logf, ckv, kr = (b_major(g[n]) for n in ("kf", "vf", "logf", "ckv", "kr"))
        qf = b_major(g["qf"])
        qf_bd = jnp.where(head_of_lane[None, None, None, :] == jnp.arange(N_HEADS)[None, :, None, None],
                          qf[:, None, :, :], jnp.zeros((), BF16)).reshape(nbs, N_HEADS * t_s, W_ATT)
        qlat = b_major(g["qlat"]).reshape(nbs, t_s, N_HEADS, LORA)
        qr = b_major(g["qr"]).reshape(nbs, t_s, N_HEADS, ROPE_DIM)
        qm = head_rows(jnp.concatenate([qlat, qr, jnp.zeros((nbs, t_s, N_HEADS, LORA - ROPE_DIM), BF16)], axis=-1))
        cn = b_major(g["c"])
        cqn = jnp.swapaxes(cn, 1, 2).reshape(nbs, N_HEADS * t_s, 1)
        cnr = jnp.pad(jnp.swapaxes(cn, 1, 2), ((0, 0), (0, 0), (0, N_KEYS_NEW - t_s)))
        kvn = jnp.concatenate([ckv, kr, jnp.zeros((nbs, t_s, LORA - ROPE_DIM), F32)], axis=-1)
        of_s, ol_s = _attn_decode(l, page_table, qf_bd, qm, cqn, cnr, pad_keys(kf), pad_keys(vf), pad_keys(kvn),
                                  ucum, mlow, caches)
        ol_s = jnp.swapaxes(ol_s.reshape(nbs, N_HEADS, t_s, LORA), 1, 2).reshape(nbs, t_s, N_HEADS * LORA)
        xs = _back(xs, tm_major(of_s), tm_major(ol_s), g["gf"], g["gm"], g["hl"], lw, alpha, t_s * nbs)
        s_states.append((kf.reshape(nbs, t_s, N_HEADS, HEAD_DIM), vf.reshape(nbs, t_s, N_HEADS, HEAD_DIM),
                         logf, ckv, kr, g["hlast"], jnp.swapaxes(g["nconv"], 0, 1)))

    p_out = [jnp.stack(z) for z in zip(*p_states)]
    s_out = [jnp.stack(z) for z in zip(*s_states)]
    return (xp, b_major(xs), *p_out, *s_out)
```

```python
import functools
import math

import jax
import jax.numpy as jnp
from jax import lax
from jax.experimental import pallas as pl
from jax.experimental.pallas import tpu as pltpu

F32 = jnp.float32
BF16 = jnp.bfloat16

HEAD_DIM = 64
N_HEADS = 4
W_ATT = N_HEADS * HEAD_DIM
ROPE_DIM = 32
LORA = 128
W_LRU = 512
CONV_W = 4
LRU_C = 8.0
ROPE_BASE = 10000.0
LN_EPS = 1e-5
RMS_EPS = 1e-6
NEG_INF = -1e30
LOG2E = math.log2(math.e)
MLA_QSCALE = LOG2E / math.sqrt(HEAD_DIM + ROPE_DIM)
FOX_QSCALE = LOG2E / math.sqrt(HEAD_DIM)
N_SPLIT = 3

LANES = 128
SUBLANES = 8
VMEM_LIMIT = 56 * 1024 * 1024

C_FQ, C_FK, C_FV, C_FG = 0, 256, 512, 768
C_QN, C_QR, C_CKV, C_KR, C_MG = 1024, 1280, 1408, 1536, 1664
C_LX, C_LG, C_FF = 1920, 2432, 2944
NZ = 3072

TM_FRONT = 512
TQ = 256
TK = 512
PAGES_PER_CHUNK = 32
N_KEYS_NEW = 16


def _mm(a, b):
    return jnp.dot(a, b, preferred_element_type=F32)


def _mm_nt(a, b):
    return lax.dot_general(a, b, (((1,), (1,)), ((), ())), preferred_element_type=F32)


def _mm_exact(a, b):
    return jnp.dot(a, b, preferred_element_type=F32, precision=lax.Precision.HIGHEST)


def _rope(x, cos, sin_a, sin_b):
    return x * cos + pltpu.roll(x, LANES - ROPE_DIM // 2, 1) * sin_a + pltpu.roll(x, ROPE_DIM // 2, 1) * sin_b


def _silu(x):
    return x * jax.nn.sigmoid(x)


def _softplus(x):
    return jnp.maximum(x, 0.0) + jnp.log1p(jnp.exp(-jnp.abs(x)))


def _head_lanes(lane, h):
    odd = h % 2
    in_head = (lane >= HEAD_DIM) if odd else (lane < HEAD_DIM)
    return in_head, (0 if odd else HEAD_DIM)


def _front_common(x_ref, w_ref, bf_ref, gkv_ref, cos_ref, sa_ref, sb_ref, wuk_ref,
                  qf_ref, kf_ref, vf_ref, gf_ref, qlat_ref, qr_ref, ckv_ref, kr_ref, kvc_ref, gm_ref, logf_ref):
    xb = x_ref[...].astype(BF16)

    def seg(start, width):
        return _mm(xb, w_ref[:, start:start + width])

    qf_ref[...] = (seg(C_FQ, W_ATT) * FOX_QSCALE).astype(BF16)
    fk = seg(C_FK, W_ATT)
    kf_ref[...] = fk
    fv = seg(C_FV, W_ATT)
    vf_ref[...] = fv
    gf_ref[...] = _silu(seg(C_FG, W_ATT))

    qn = seg(C_QN, W_ATT).astype(BF16)
    qlat_ref[...] = (_mm(qn, wuk_ref[...]) * MLA_QSCALE).astype(BF16)
    cos, sa, sb = cos_ref[...], sa_ref[...], sb_ref[...]
    qr_ref[...] = (_rope(seg(C_QR, LANES), cos, sa, sb) * MLA_QSCALE).astype(BF16)

    mckv = seg(C_CKV, LORA)
    ms = jnp.mean(mckv * mckv, axis=-1, keepdims=True)
    ckv = mckv * lax.rsqrt(ms + RMS_EPS) * gkv_ref[...]
    ckv_ref[...] = ckv
    ckvb = ckv.astype(BF16)
    kvc_ref[:, 0:LORA] = ckvb
    krt = _rope(seg(C_KR, LANES), cos, sa, sb)
    kr_ref[...] = krt[:, 0:ROPE_DIM]
    kvc_ref[:, LORA:2 * LORA] = krt.astype(BF16)
    gm_ref[...] = _silu(seg(C_MG, W_ATT))

    y = seg(C_FF, LANES) + bf_ref[...]
    ls = jnp.minimum(y, 0.0) - jnp.log1p(jnp.exp(-jnp.abs(y)))
    logf_ref[...] = ls[:, 0:N_HEADS]

    return seg(C_LX, W_LRU), seg(C_LG, W_LRU), ls, fk, fv, ckvb


def _lru_coeffs(xc, wga_ref, wgx_ref, bga_ref, bgx_ref, lam_ref):
    xcb = xc.astype(BF16)
    half = W_LRU // 2

    def gate(w_ref, b_ref):
        g = jnp.concatenate([_mm(xcb[:, :half], w_ref[0]), _mm(xcb[:, half:], w_ref[1])], axis=-1)
        return jax.nn.sigmoid(g + b_ref[...])

    r = gate(wga_ref, bga_ref)
    i = gate(wgx_ref, bgx_ref)
    log_a = -LRU_C * r * _softplus(-lam_ref[...])
    a = jnp.exp(log_a)
    b = jnp.sqrt(1.0 - jnp.exp(2.0 * log_a)) * (i * xc)
    return a, b


def _front_prompt_kernel(x_ref, w_ref, bf_ref, gkv_ref, cos_ref, sa_ref, sb_ref, wuk_ref,
                         cw_ref, cb_ref, wga_ref, wgx_ref, bga_ref, bgx_ref, lam_ref, tri_ref,
                         qf_ref, kf_ref, vf_ref, gf_ref, qlat_ref, qr_ref, ckv_ref, kr_ref, kvc_ref, gm_ref, logf_ref,
                         c2_ref, kaug_ref, vaug_ref, vm_ref, hl_ref, hlast_ref, nconv_ref,
                         xpad_scr, a_scr, b_scr, h_scr, hcar_scr, ccar_scr, *, tm):
    i = pl.program_id(1)

    @pl.when(i == 0)
    def _():
        xpad_scr[0:SUBLANES, :] = jnp.zeros((SUBLANES, W_LRU), F32)
        hcar_scr[...] = jnp.zeros_like(hcar_scr)
        ccar_scr[...] = jnp.zeros_like(ccar_scr)

    lx, lg, ls, fk, fv, ckvb = _front_common(
        x_ref, w_ref, bf_ref, gkv_ref, cos_ref, sa_ref, sb_ref, wuk_ref,
        qf_ref, kf_ref, vf_ref, gf_ref, qlat_ref, qr_ref, ckv_ref, kr_ref, kvc_ref, gm_ref, logf_ref)

    c = _mm_exact(tri_ref[...], ls) + ccar_scr[...]
    ccar_scr[...] = c[tm - 1:tm, :]
    c2 = c * LOG2E
    c2_ref[...] = c2[:, 0:N_HEADS]

    lane = lax.broadcasted_iota(jnp.int32, (tm, LANES), 1)
    for h in range(N_HEADS):
        in_head, spare = _head_lanes(lane, h)
        pair = slice((h // 2) * LANES, (h // 2 + 1) * LANES)
        rest = -c2[:, h:h + 1]
        aux = jnp.zeros((tm, LANES), F32)
        for k in range(N_SPLIT):
            piece = rest.astype(BF16).astype(F32)
            aux = jnp.where(lane == spare + k, piece, aux)
            rest = rest - piece
        kaug_ref[:, h * LANES:(h + 1) * LANES] = jnp.where(in_head, fk[:, pair], aux).astype(BF16)
        vaug_ref[:, h * LANES:(h + 1) * LANES] = jnp.where(in_head, fv[:, pair], 1.0).astype(BF16)
    vm_ref[:, 0:LORA] = ckvb
    vm_ref[:, LORA:2 * LORA] = jnp.ones((tm, LORA), BF16)

    xpad_scr[SUBLANES:SUBLANES + tm, :] = lx
    xc = cb_ref[...] + xpad_scr[SUBLANES:SUBLANES + tm, :] * cw_ref[CONV_W - 1:CONV_W, :]
    for j in range(1, CONV_W):
        xc = xc + xpad_scr[SUBLANES - j:SUBLANES - j + tm, :] * cw_ref[CONV_W - 1 - j:CONV_W - j, :]
    tail = xpad_scr[tm:tm + SUBLANES, :]
    xpad_scr[0:SUBLANES, :] = tail
    nconv_ref[...] = tail[SUBLANES - (CONV_W - 1):, :]

    a, b = _lru_coeffs(xc, wga_ref, wgx_ref, bga_ref, bgx_ref, lam_ref)
    a_scr[...] = a
    b_scr[...] = b

    sub = lax.broadcasted_iota(jnp.int32, (SUBLANES, W_LRU), 0)

    def group(g, carry):
        r0 = pl.multiple_of(g * SUBLANES, SUBLANES)
        a8 = a_scr[pl.ds(r0, SUBLANES), :]
        b8 = b_scr[pl.ds(r0, SUBLANES), :]
        for d in (1, 2, 4):
            keep = sub >= d
            b8 = jnp.where(keep, a8 * pltpu.roll(b8, d, 0) + b8, b8)
            a8 = jnp.where(keep, a8 * pltpu.roll(a8, d, 0), a8)
        h8 = a8 * carry + b8
        h_scr[pl.ds(r0, SUBLANES), :] = h8
        return h8[SUBLANES - 1:SUBLANES, :]

    carry = lax.fori_loop(0, tm // SUBLANES, group, hcar_scr[...], unroll=4)
    hcar_scr[...] = carry
    hlast_ref[...] = carry
    hl_ref[...] = h_scr[...] * _silu(lg)


def _front_sample_kernel(x_ref, w_ref, bf_ref, gkv_ref, cos_ref, sa_ref, sb_ref, wuk_ref,
                         cw_ref, cb_ref, wga_ref, wgx_ref, bga_ref, bgx_ref, lam_ref, h0_ref, cbuf_ref,
                         qf_ref, kf_ref, vf_ref, gf_ref, qlat_ref, qr_ref, ckv_ref, kr_ref, kvc_ref, gm_ref, logf_ref,
                         cn_ref, hl_ref, hlast_ref, nconv_ref, *, nb, nt):
    lx, lg, ls, _, _, _ = _front_common(
        x_ref, w_ref, bf_ref, gkv_ref, cos_ref, sa_ref, sb_ref, wuk_ref,
        qf_ref, kf_ref, vf_ref, gf_ref, qlat_ref, qr_ref, ckv_ref, kr_ref, kvc_ref, gm_ref, logf_ref)
    xs = [cbuf_ref[j] for j in range(CONV_W - 1)] + [lx[t * nb:(t + 1) * nb, :] for t in range(nt)]
    xc = jnp.concatenate(
        [cb_ref[...] + sum(xs[t + j] * cw_ref[j:j + 1, :] for j in range(CONV_W)) for t in range(nt)], axis=0)
    for j in range(CONV_W - 1):
        nconv_ref[j] = xs[nt + j]

    a, b = _lru_coeffs(xc, wga_ref, wgx_ref, bga_ref, bgx_ref, lam_ref)
    h = h0_ref[...]
    cn = jnp.zeros((nb, N_HEADS), F32)
    hs = []
    for t in range(nt):
        h = a[t * nb:(t + 1) * nb, :] * h + b[t * nb:(t + 1) * nb, :]
        hs.append(h)
        cn = cn + ls[t * nb:(t + 1) * nb, 0:N_HEADS]
        cn_ref[t * nb:(t + 1) * nb, :] = cn
    hlast_ref[...] = h
    hl_ref[...] = jnp.concatenate(hs, axis=0) * _silu(lg)


def _row_spec(tm, c):
    return pl.BlockSpec((None, tm, c), lambda b, i: (b, i, 0))


def _const_spec(shape, ngrid):
    zeros = (0,) * len(shape)
    if ngrid == 2:
        return pl.BlockSpec(shape, lambda b, i: zeros)
    return pl.BlockSpec(shape, lambda i: zeros)


_FRONT_COMMON_COLS = (
    ("qf", W_ATT, BF16), ("kf", W_ATT, F32), ("vf", W_ATT, F32), ("gf", W_ATT, F32),
    ("qlat", N_HEADS * LORA, BF16), ("qr", LANES, BF16), ("ckv", LORA, F32), ("kr", ROPE_DIM, F32),
    ("kvc", 2 * LORA, BF16), ("gm", W_ATT, F32), ("logf", N_HEADS, F32))
_FRONT_PROMPT_COLS = _FRONT_COMMON_COLS + (
    ("c2", N_HEADS, F32), ("kaug", N_HEADS * LANES, BF16), ("vaug", N_HEADS * LANES, BF16),
    ("vm", 2 * LORA, BF16), ("hl", W_LRU, F32))
_FRONT_SAMPLE_COLS = _FRONT_COMMON_COLS + (("cn", N_HEADS, F32), ("hl", W_LRU, F32))


def _front_prompt(x, lw, tables, tri):
    nb, t, d = x.shape
    tm = TM_FRONT
    consts = [lw["w_in"], lw["b_f"], lw["g_kv"]]
    in_specs = [_row_spec(tm, d), _const_spec(lw["w_in"].shape, 2), _const_spec((1, LANES), 2),
                _const_spec((1, LANES), 2)]
    in_specs += [pl.BlockSpec((tm, LANES), lambda b, i: (i, 0))] * 3
    rest = [lw["wuk_bd"], lw["conv_w"], lw["conv_b"], lw["wga"], lw["wgx"], lw["b_gate_a"], lw["b_gate_x"],
            lw["lam"], tri]
    in_specs += [_const_spec(r.shape, 2) for r in rest]
    out_shape = [jax.ShapeDtypeStruct((nb, t, c), dt) for _, c, dt in _FRONT_PROMPT_COLS]
    out_specs = [_row_spec(tm, c) for _, c, _ in _FRONT_PROMPT_COLS]
    out_shape += [jax.ShapeDtypeStruct((nb, 1, W_LRU), F32), jax.ShapeDtypeStruct((nb, CONV_W - 1, W_LRU), F32)]
    out_specs += [pl.BlockSpec((None, 1, W_LRU), lambda b, i: (b, 0, 0)),
                  pl.BlockSpec((None, CONV_W - 1, W_LRU), lambda b, i: (b, 0, 0))]
    outs = pl.pallas_call(
        functools.partial(_front_prompt_kernel, tm=tm),
        grid=(nb, t // tm),
        in_specs=in_specs, out_specs=out_specs, out_shape=out_shape,
        scratch_shapes=[pltpu.VMEM((tm + SUBLANES, W_LRU), F32), pltpu.VMEM((tm, W_LRU), F32),
                        pltpu.VMEM((tm, W_LRU), F32), pltpu.VMEM((tm, W_LRU), F32),
                        pltpu.VMEM((1, W_LRU), F32), pltpu.VMEM((1, LANES), F32)],
        compiler_params=pltpu.CompilerParams(dimension_semantics=("arbitrary", "arbitrary"),
                                             vmem_limit_bytes=VMEM_LIMIT),
        name="front_prompt",
    )(x, *consts, *tables, *rest)
    names = [n for n, _, _ in _FRONT_PROMPT_COLS] + ["hlast", "nconv"]
    return dict(zip(names, outs))


def _front_sample(x, lw, tables, h0, cbuf, nb, nt):
    rows, d = x.shape
    args = [x, lw["w_in"], lw["b_f"], lw["g_kv"], *tables, lw["wuk_bd"], lw["conv_w"], lw["conv_b"], lw["wga"],
            lw["wgx"], lw["b_gate_a"], lw["b_gate_x"], lw["lam"], h0, cbuf]
    in_specs = [_const_spec(a.shape, 1) for a in args]
    out_shape = [jax.ShapeDtypeStruct((rows, c), dt) for _, c, dt in _FRONT_SAMPLE_COLS]
    out_shape += [jax.ShapeDtypeStruct((nb, W_LRU), F32), jax.ShapeDtypeStruct((CONV_W - 1, nb, W_LRU), F32)]
    out_specs = [_const_spec(s.shape, 1) for s in out_shape]
    outs = pl.pallas_call(
        functools.partial(_front_sample_kernel, nb=nb, nt=nt),
        grid=(1,),
        in_specs=in_specs, out_specs=out_specs, out_shape=out_shape,
        compiler_params=pltpu.CompilerParams(dimension_semantics=("arbitrary",), vmem_limit_bytes=VMEM_LIMIT),
        name="front_sample",
    )(*args)
    names = [n for n, _, _ in _FRONT_SAMPLE_COLS] + ["hlast", "nconv"]
    return dict(zip(names, outs))


def _attn_prompt_kernel(qf_ref, c2_ref, kaug_ref, vaug_ref, qlat_ref, qr_ref, kvc_ref, vm_ref,
                        of_ref, ol_ref,
                        qfs, qms, cqs, mf, accf, mm, accm, *, tq, tk):
    i = pl.program_id(1)
    n_full = (i * tq) // tk
    diag_off = i * tq - n_full * tk
    rows = N_HEADS * tq
    lane = lax.broadcasted_iota(jnp.int32, (tq, LANES), 1)

    qr = qr_ref[...]
    for h in range(N_HEADS):
        in_head, spare = _head_lanes(lane, h)
        blk = slice(h * tq, (h + 1) * tq)
        ones = jnp.where(lane >= spare, jnp.where(lane < spare + N_SPLIT, 1.0, 0.0), 0.0).astype(BF16)
        qfs[blk, :] = jnp.where(in_head, qf_ref[:, (h // 2) * LANES:(h // 2 + 1) * LANES], ones)
        cqs[blk, :] = c2_ref[:, h:h + 1]
        qms[blk, 0:LORA] = qlat_ref[:, h * LORA:(h + 1) * LORA]
        qms[blk, LORA:2 * LORA] = jnp.where(lane // ROPE_DIM == h, qr, jnp.zeros_like(qr))
    mf[...] = jnp.full(mf.shape, NEG_INF, F32)
    mm[...] = jnp.full(mm.shape, NEG_INF, F32)
    accf[...] = jnp.zeros_like(accf)
    accm[...] = jnp.zeros_like(accm)

    def step(j, masked):
        off = pl.multiple_of(j * tk, tk)
        if masked:
            d = (lax.broadcasted_iota(jnp.int32, (rows, tk), 1)
                 - (lax.broadcasted_iota(jnp.int32, (rows, tk), 0) & (tq - 1)))
            visible = d <= diag_off

        s = jnp.concatenate(
            [_mm_nt(qfs[h * tq:(h + 1) * tq, :], kaug_ref[pl.ds(off, tk), h * LANES:(h + 1) * LANES])
             for h in range(N_HEADS)], axis=0)
        if masked:
            s = jnp.where(visible, s, NEG_INF)
        cq = cqs[...]
        m_old = mf[...]
        m_new = jnp.maximum(m_old, jnp.max(s, axis=-1, keepdims=True) + cq)
        p = jnp.exp2(s + (cq - m_new)).astype(BF16)
        pv = jnp.concatenate(
            [_mm(p[h * tq:(h + 1) * tq, :], vaug_ref[pl.ds(off, tk), h * LANES:(h + 1) * LANES])
             for h in range(N_HEADS)], axis=0)
        accf[...] = jnp.exp2(m_old - m_new) * accf[...] + pv
        mf[...] = m_new

        s = _mm_nt(qms[...], kvc_ref[pl.ds(off, tk), :])
        if masked:
            s = jnp.where(visible, s, NEG_INF)
        m_old = mm[...]
        m_new = jnp.maximum(m_old, jnp.max(s, axis=-1, keepdims=True))
        p = jnp.exp2(s - m_new).astype(BF16)
        accm[...] = jnp.exp2(m_old - m_new) * accm[...] + _mm(p, vm_ref[pl.ds(off, tk), :])
        mm[...] = m_new

    def body(j, carry):
        step(j, False)
        return carry

    lax.fori_loop(0, n_full, body, 0)
    step(n_full, True)

    for pair in range(N_HEADS // 2):
        o = []
        for h in (2 * pair, 2 * pair + 1):
            acc = accf[h * tq:(h + 1) * tq, :]
            o.append(acc * (1.0 / pltpu.roll(acc, HEAD_DIM, 1)))
        of_ref[:, pair * LANES:(pair + 1) * LANES] = jnp.where(lane < HEAD_DIM, o[0], o[1])
    for h in range(N_HEADS):
        acc = accm[h * tq:(h + 1) * tq, :]
        ol_ref[:, h * LORA:(h + 1) * LORA] = acc[:, 0:LORA] * (1.0 / acc[:, LORA:2 * LORA])


def _attn_prompt(f):
    nb, t, _ = f["qf"].shape
    tq, tk = TQ, TK
    assert tk % tq == 0 and tq & (tq - 1) == 0 and t % tk == 0
    rows = N_HEADS * tq
    qspec = lambda c: pl.BlockSpec((None, tq, c), lambda b, i: (b, i, 0))
    fullspec = lambda c: pl.BlockSpec((None, t, c), lambda b, i: (b, 0, 0))
    return pl.pallas_call(
        functools.partial(_attn_prompt_kernel, tq=tq, tk=tk),
        grid=(nb, t // tq),
        in_specs=[qspec(W_ATT), qspec(N_HEADS), fullspec(N_HEADS * LANES), fullspec(N_HEADS * LANES),
                  qspec(N_HEADS * LORA), qspec(LANES), fullspec(2 * LORA), fullspec(2 * LORA)],
        out_specs=[qspec(W_ATT), qspec(N_HEADS * LORA)],
        out_shape=[jax.ShapeDtypeStruct((nb, t, W_ATT), F32), jax.ShapeDtypeStruct((nb, t, N_HEADS * LORA), F32)],
        scratch_shapes=[pltpu.VMEM((rows, LANES), BF16), pltpu.VMEM((rows, 2 * LORA), BF16),
                        pltpu.VMEM((rows, 1), F32), pltpu.VMEM((rows, 1), F32), pltpu.VMEM((rows, LANES), F32),
                        pltpu.VMEM((rows, 1), F32), pltpu.VMEM((rows, 2 * LORA), F32)],
        compiler_params=pltpu.CompilerParams(dimension_semantics=("arbitrary", "arbitrary"),
                                             vmem_limit_bytes=VMEM_LIMIT),
        name="attn_prompt",
    )(f["qf"], f["c2"], f["kaug"], f["vaug"], f["qlat"], f["qr"], f["kvc"], f["vm"])


def _online_update(carry, t, cq, pv_fn):
    m, l, acc = carry
    row_max = jnp.max(t, axis=-1, keepdims=True)
    m_new = jnp.maximum(m, row_max if cq is None else row_max + cq)
    shift = -m_new if cq is None else cq - m_new
    p = jnp.exp2(t + shift)
    alpha = jnp.exp2(m - m_new)
    l_new = alpha * l + jnp.sum(p, axis=-1, keepdims=True)
    acc_new = alpha * acc + pv_fn(p.astype(BF16))
    return m_new, l_new, acc_new


def _attn_decode_kernel(pt_ref, qf_ref, qm_ref, cqn_ref, cnr_ref, kn_ref, vn_ref, kvn_ref, ucum_ref, mlow_ref,
                        ck_hbm, cv_hbm, clf_hbm, cckv_hbm, ckr_hbm,
                        of_ref, ol_ref,
                        kbuf, vbuf, cbuf, rbuf, lfbuf, cp_scr, sem, lsem,
                        *, layer, nb, n_pages, page, nq):
    b = pl.program_id(0)
    ppc = PAGES_PER_CHUNK
    n_chunks = n_pages // ppc
    rows = N_HEADS * nq

    def chunk_copies(bb, c, slot):
        cps = []
        for p in range(ppc):
            pg = pt_ref[bb, c * ppc + p]
            dst = pl.ds(p * page, page)
            cps.append(pltpu.make_async_copy(ck_hbm.at[layer, pg], kbuf.at[slot, :, dst], sem.at[0, slot]))
            cps.append(pltpu.make_async_copy(cv_hbm.at[layer, pg], vbuf.at[slot, :, dst], sem.at[1, slot]))
            cps.append(pltpu.make_async_copy(cckv_hbm.at[layer, pg], cbuf.at[slot, dst, :], sem.at[2, slot]))
            cps.append(pltpu.make_async_copy(ckr_hbm.at[layer, pg], rbuf.at[slot, :, dst], sem.at[3, slot]))
        return cps

    def logf_copies(bb, slot):
        return [pltpu.make_async_copy(clf_hbm.at[layer, pt_ref[bb, p]], lfbuf.at[slot, :, pl.ds(p, 1), :],
                                      lsem.at[slot]) for p in range(n_pages)]

    lslot = b % 2

    @pl.when(b == 0)
    def _():
        for cp in logf_copies(0, 0):
            cp.start()
        for cp in chunk_copies(0, 0, 0):
            cp.start()

    @pl.when(b + 1 < nb)
    def _():
        for cp in logf_copies(b + 1, 1 - lslot):
            cp.start()

    for cp in logf_copies(b, lslot):
        cp.wait()

    lf = lfbuf[lslot].reshape(N_HEADS * n_pages, page)
    within = _mm_exact(lf, ucum_ref[...])
    tot = jnp.broadcast_to(within[:, page - 1:page], within.shape)
    cpast = (within + _mm_exact(mlow_ref[...], tot)) * LOG2E
    cp_scr[...] = cpast
    ctot = jnp.concatenate(
        [jnp.broadcast_to(cpast[(h + 1) * n_pages - 1:(h + 1) * n_pages, page - 1:page], (nq, 1))
         for h in range(N_HEADS)], axis=0)
    cq = ctot + cqn_ref[...] * LOG2E

    qf = qf_ref[...]
    qm = qm_ref[...]
    fox = (jnp.full((rows, 1), NEG_INF, F32), jnp.zeros((rows, 1), F32), jnp.zeros((rows, W_ATT), F32))
    mla = (jnp.full((rows, 1), NEG_INF, F32), jnp.zeros((rows, 1), F32), jnp.zeros((rows, LORA), F32))

    for c in range(n_chunks):
        slot = c % 2
        if c + 1 < n_chunks:
            for cp in chunk_copies(b, c + 1, 1 - slot):
                cp.start()
        else:
            @pl.when(b + 1 < nb)
            def _():
                for cp in chunk_copies(b + 1, 0, 1 - slot):
                    cp.start()
        for cp in chunk_copies(b, c, slot):
            cp.wait()

        kt = kbuf[slot].astype(BF16)
        vt = vbuf[slot].astype(BF16)
        bias = jnp.concatenate(
            [jnp.concatenate([jnp.broadcast_to(cp_scr[h * n_pages + c * ppc + p:h * n_pages + c * ppc + p + 1, :],
                                               (nq, page)) for h in range(N_HEADS)], axis=0)
             for p in range(ppc)], axis=1)
        fox = _online_update(fox, _mm(qf, kt) - bias, cq, lambda p: _mm_nt(p, vt))

        cb = cbuf[slot].astype(BF16)
        rt = rbuf[slot].astype(BF16)
        s = _mm_nt(qm[:, 0:LORA], cb) + _mm(qm[:, LORA:LORA + ROPE_DIM], rt)
        mla = _online_update(mla, s, None, lambda p: _mm(p, cb))

    rq = lax.broadcasted_iota(jnp.int32, (rows, N_KEYS_NEW), 0) & (nq - 1)
    visible = lax.broadcasted_iota(jnp.int32, (rows, N_KEYS_NEW), 1) <= rq
    knb = kn_ref[...].astype(BF16)
    vnb = vn_ref[...].astype(BF16)
    cnew = jnp.concatenate([jnp.broadcast_to(cnr_ref[h:h + 1, :], (nq, N_KEYS_NEW)) for h in range(N_HEADS)], axis=0)
    t = jnp.where(visible, _mm_nt(qf, knb) - (ctot + cnew * LOG2E), NEG_INF)
    _, l, acc = _online_update(fox, t, cq, lambda p: _mm(p, vnb))
    o = acc * (1.0 / l)
    lane = lax.broadcasted_iota(jnp.int32, (nq, W_ATT), 1)
    of = jnp.zeros((nq, W_ATT), F32)
    for h in range(N_HEADS):
        of = jnp.where(lane // HEAD_DIM == h, o[h * nq:(h + 1) * nq, :], of)
    of_ref[...] = of

    kvn = kvn_ref[...].astype(BF16)
    t = jnp.where(visible, _mm_nt(qm, kvn), NEG_INF)
    _, l, acc = _online_update(mla, t, None, lambda p: _mm(p, kvn[:, 0:LORA]))
    ol_ref[...] = acc * (1.0 / l)


def _attn_decode(layer, page_table, qf_bd, qm, cqn, cnr, kn, vn, kvn, ucum, mlow, caches):
    nb, n_pages = page_table.shape
    ck, cv, clf, cckv, ckr = caches
    page = ck.shape[3]
    rows = qf_bd.shape[1]
    nq = rows // N_HEADS
    ck_keys = PAGES_PER_CHUNK * page
    assert n_pages % (2 * PAGES_PER_CHUNK) == 0 and nq & (nq - 1) == 0 and n_pages % SUBLANES == 0
    bspec = lambda r, c: pl.BlockSpec((None, r, c), lambda b, pt: (b, 0, 0))
    cspec = lambda s: pl.BlockSpec(s, lambda b, pt: (0,) * len(s))
    anyspec = pl.BlockSpec(memory_space=pl.ANY)
    grid_spec = pltpu.PrefetchScalarGridSpec(
        num_scalar_prefetch=1,
        grid=(nb,),
        in_specs=[bspec(rows, W_ATT), bspec(rows, 2 * LORA), bspec(rows, 1), bspec(N_HEADS, N_KEYS_NEW),
                  bspec(N_KEYS_NEW, W_ATT), bspec(N_KEYS_NEW, W_ATT), bspec(N_KEYS_NEW, 2 * LORA),
                  cspec(ucum.shape), cspec(mlow.shape), anyspec, anyspec, anyspec, anyspec, anyspec],
        out_specs=[bspec(nq, W_ATT), bspec(rows, LORA)],
        scratch_shapes=[pltpu.VMEM((2, W_ATT, ck_keys), F32), pltpu.VMEM((2, W_ATT, ck_keys), F32),
                        pltpu.VMEM((2, ck_keys, LORA), F32), pltpu.VMEM((2, ROPE_DIM, ck_keys), F32),
                        pltpu.VMEM((2, N_HEADS, n_pages, page), F32),
                        pltpu.VMEM((N_HEADS * n_pages, page), F32),
                        pltpu.SemaphoreType.DMA((4, 2)), pltpu.SemaphoreType.DMA((2,))])
    return pl.pallas_call(
        functools.partial(_attn_decode_kernel, layer=layer, nb=nb, n_pages=n_pages, page=page, nq=nq),
        grid_spec=grid_spec,
        out_shape=[jax.ShapeDtypeStruct((nb, nq, W_ATT), F32), jax.ShapeDtypeStruct((nb, rows, LORA), F32)],
        compiler_params=pltpu.CompilerParams(dimension_semantics=("arbitrary",), vmem_limit_bytes=VMEM_LIMIT),
        name="attn_decode",
    )(page_table, qf_bd, qm, cqn, cnr, kn, vn, kvn, ucum, mlow, ck, cv, clf, cckv, ckr)


def _back_kernel(x_ref, of_ref, ol_ref, gf_ref, gm_ref, hl_ref, wuv_ref, wout_ref, g_ref, b_ref, y_ref, *, alpha):
    o_mla = _mm(ol_ref[...].astype(BF16), wuv_ref[...])
    mix = jnp.concatenate([of_ref[...] * gf_ref[...], o_mla * gm_ref[...], hl_ref[...]], axis=-1).astype(BF16)
    r = alpha * x_ref[...] + _mm(mix, wout_ref[...])
    mu = jnp.mean(r, axis=-1, keepdims=True)
    rc = r - mu
    var = jnp.mean(rc * rc, axis=-1, keepdims=True)
    y_ref[...] = rc * lax.rsqrt(var + LN_EPS) * g_ref[...] + b_ref[...]


def _back(x, of, ol, gf, gm, hl, lw, alpha, tm):
    rows, d = x.shape
    rspec = lambda c: pl.BlockSpec((tm, c), lambda i: (i, 0))
    consts = [lw["wuv_bd"], lw["w_out"], lw["ln_g"], lw["ln_b"]]
    return pl.pallas_call(
        functools.partial(_back_kernel, alpha=alpha),
        grid=(rows // tm,),
        in_specs=[rspec(d), rspec(W_ATT), rspec(N_HEADS * LORA), rspec(W_ATT), rspec(W_ATT), rspec(W_LRU)]
        + [_const_spec(c.shape, 1) for c in consts],
        out_specs=rspec(d),
        out_shape=jax.ShapeDtypeStruct((rows, d), F32),
        compiler_params=pltpu.CompilerParams(dimension_semantics=("arbitrary",), vmem_limit_bytes=VMEM_LIMIT),
        name="back",
    )(x, of, ol, gf, gm, hl, *consts)


def _rope_tables(pos):
    half = ROPE_DIM // 2
    lane = jnp.arange(LANES)
    inv = ROPE_BASE ** (-jnp.arange(half, dtype=F32) / half)
    ang = pos.astype(F32)[:, None] * inv[lane % half][None, :]
    first = ((lane % ROPE_DIM) < half)[None, :]
    sin = jnp.sin(ang)
    return jnp.cos(ang), jnp.where(first, -sin, 0.0), jnp.where(first, 0.0, sin)


def _block_diag(blocks):
    n, r, c = blocks.shape
    eye = jnp.eye(n, dtype=blocks.dtype)
    return (eye[:, None, :, None] * blocks[:, :, None, :]).reshape(n * r, n * c)


def _prep_layer(l, w_in, b_f, g_kv, w_uk, w_uv, conv_w, conv_b, w_gate_a, b_gate_a, w_gate_x, b_gate_x,
                lru_lambda, w_out, ln_g, ln_b):
    wi = w_in[l]
    o = 0
    parts = {}
    for name, width in (("fq", 256), ("fk", 256), ("fv", 256), ("ff", 4), ("fg", 256), ("qn", 256), ("qr", 128),
                        ("ckv", 128), ("kr", 32), ("mg", 256), ("lx", 512), ("lg", 512)):
        parts[name] = wi[:, o:o + width]
        o += width
    d = wi.shape[0]
    w_cat = jnp.concatenate(
        [parts["fq"], parts["fk"], parts["fv"], parts["fg"], parts["qn"], parts["qr"], parts["ckv"],
         parts["kr"], parts["kr"], parts["kr"], parts["kr"], parts["mg"], parts["lx"], parts["lg"],
         parts["ff"], jnp.zeros((d, LANES - N_HEADS), wi.dtype)], axis=1).astype(BF16)
    assert w_cat.shape[1] == NZ
    row = lambda v: v.reshape(1, -1).astype(F32)
    nblk = w_gate_a.shape[1]
    gate_bd = lambda w: jnp.stack([_block_diag(w[:nblk // 2]), _block_diag(w[nblk // 2:])]).astype(BF16)
    return {
        "w_in": w_cat,
        "b_f": jnp.pad(row(b_f[l]), ((0, 0), (0, LANES - N_HEADS))),
        "g_kv": row(g_kv[l]),
        "wuk_bd": _block_diag(jnp.transpose(w_uk[l], (1, 2, 0))).astype(BF16),
        "wuv_bd": _block_diag(jnp.transpose(w_uv[l], (1, 0, 2))).astype(BF16),
        "conv_w": conv_w[l].astype(F32), "conv_b": row(conv_b[l]),
        "wga": gate_bd(w_gate_a[l]), "wgx": gate_bd(w_gate_x[l]),
        "b_gate_a": row(b_gate_a[l]), "b_gate_x": row(b_gate_x[l]), "lam": row(lru_lambda[l]),
        "w_out": w_out[l].astype(BF16), "ln_g": row(ln_g[l]), "ln_b": row(ln_b[l]),
    }


def kernel(x_prompt, x_sample, cache_fox_k, cache_fox_v, cache_fox_logf, cache_mla_ckv, cache_mla_krope,
           state_lru_h, state_lru_conv, page_table, w_in, b_f, g_kv, w_uk, w_uv, conv_w, conv_b,
           w_gate_a, b_gate_a, w_gate_x, b_gate_x, lru_lambda, w_out, ln_g, ln_b):
    depth = w_in.shape[0]
    nbp, t_p, d = x_prompt.shape
    nbs, t_s, _ = x_sample.shape
    n_pool, page = cache_fox_k.shape[1], cache_fox_k.shape[2]
    n_pages = page_table.shape[1]
    past_len = n_pages * page
    alpha = (2 * depth) ** 0.25
    assert t_s == SUBLANES and t_s <= N_KEYS_NEW

    tables_p = _rope_tables(jnp.arange(t_p, dtype=jnp.int32))
    tables_s = _rope_tables(past_len + jnp.repeat(jnp.arange(t_s, dtype=jnp.int32), nbs))
    tri = jnp.tril(jnp.ones((TM_FRONT, TM_FRONT), F32))
    ucum = jnp.triu(jnp.ones((page, page), F32))
    idx = jnp.arange(N_HEADS * n_pages)
    mlow = ((idx[:, None] // n_pages == idx[None, :] // n_pages) & (idx[None, :] < idx[:, None])).astype(F32)

    feat_major = lambda a: jnp.transpose(a, (0, 1, 3, 4, 2)).reshape(depth, n_pool, W_ATT, page)
    caches = (feat_major(cache_fox_k), feat_major(cache_fox_v),
              jnp.swapaxes(cache_fox_logf, 2, 3).reshape(depth, n_pool, N_HEADS, 1, page),
              cache_mla_ckv, jnp.swapaxes(cache_mla_krope, 2, 3))

    tm_major = lambda a: jnp.swapaxes(a, 0, 1).reshape(t_s * nbs, *a.shape[2:])
    b_major = lambda a: jnp.swapaxes(a.reshape(t_s, nbs, *a.shape[1:]), 0, 1)
    head_rows = lambda a: jnp.swapaxes(a, 1, 2).reshape(nbs, N_HEADS * t_s, a.shape[-1])
    pad_keys = lambda a: jnp.pad(a, ((0, 0), (0, N_KEYS_NEW - t_s), (0, 0)))
    head_of_lane = jnp.arange(W_ATT) // HEAD_DIM

    xp = x_prompt
    xs = tm_major(x_sample)
    p_states, s_states = [], []
    for l in range(depth):
        lw = _prep_layer(l, w_in, b_f, g_kv, w_uk, w_uv, conv_w, conv_b, w_gate_a, b_gate_a, w_gate_x, b_gate_x,
                         lru_lambda, w_out, ln_g, ln_b)

        f = _front_prompt(xp, lw, tables_p, tri)
        of, ol = _attn_prompt(f)
        flat = lambda a: a.reshape(nbp * t_p, a.shape[-1])
        xp = _back(flat(xp), flat(of), flat(ol), flat(f["gf"]), flat(f["gm"]), flat(f["hl"]), lw, alpha,
                   TM_FRONT).reshape(nbp, t_p, d)
        p_states.append((f["kf"].reshape(nbp, t_p, N_HEADS, HEAD_DIM), f["vf"].reshape(nbp, t_p, N_HEADS, HEAD_DIM),
                         f["logf"], f["ckv"], f["kr"], f["hlast"].reshape(nbp, W_LRU), f["nconv"]))

        g = _front_sample(xs, lw, tables_s, state_lru_h[l], jnp.swapaxes(state_lru_conv[l], 0, 1), nbs, t_s)
        kf, vf, logf, ckv, kr = (b_major(g[n]) for n in ("kf", "vf", "logf", "ckv", "kr"))
        qf = b_major(g["qf"])
        qf_bd = jnp.where(head_of_lane[None, None, None, :] == jnp.arange(N_HEADS)[None, :, None, None],
                          qf[:, None, :, :], jnp.zeros((), BF16)).reshape(nbs, N_HEADS * t_s, W_ATT)
        qlat = b_major(g["qlat"]).reshape(nbs, t_s, N_HEADS, LORA)
        qr = b_major(g["qr"]).reshape(nbs, t_s, N_HEADS, ROPE_DIM)
        qm = head_rows(jnp.concatenate([qlat, qr, jnp.zeros((nbs, t_s, N_HEADS, LORA - ROPE_DIM), BF16)], axis=-1))
        cn = b_major(g["cn"])
        cqn = jnp.swapaxes(cn, 1, 2).reshape(nbs, N_HEADS * t_s, 1)
        cnr = jnp.pad(jnp.swapaxes(cn, 1, 2), ((0, 0), (0, 0), (0, N_KEYS_NEW - t_s)))
        kvn = jnp.concatenate([ckv, kr, jnp.zeros((nbs, t_s, LORA - ROPE_DIM), F32)], axis=-1)
        of_s, ol_s = _attn_decode(l, page_table, qf_bd, qm, cqn, cnr, pad_keys(kf), pad_keys(vf), pad_keys(kvn),
                                  ucum, mlow, caches)
        ol_s = jnp.swapaxes(ol_s.reshape(nbs, N_HEADS, t_s, LORA), 1, 2).reshape(nbs, t_s, N_HEADS * LORA)
        xs = _back(xs, tm_major(of_s), tm_major(ol_s), g["gf"], g["gm"], g["hl"], lw, alpha, t_s * nbs)
        s_states.append((kf.reshape(nbs, t_s, N_HEADS, HEAD_DIM), vf.reshape(nbs, t_s, N_HEADS, HEAD_DIM),
                         logf, ckv, kr, g["hlast"], jnp.swapaxes(g["nconv"], 0, 1)))

    p_out = [jnp.stack(z) for z in zip(*p_states)]
    s_out = [jnp.stack(z) for z in zip(*s_states)]
    return (xp, b_major(xs), *p_out, *s_out)
```
